```python
import jax, jax.numpy as jnp
from jax import lax
import numpy as np

D_MODEL = 2048
BATCH = 2
SEQ = 16384
DEPTH = 2

HEAD_DIM = 128
N_HEADS = D_MODEL // HEAD_DIM
MLA_HEADS = N_HEADS // 2
DIL_HEADS = N_HEADS - MLA_HEADS
SB_HEADS = N_HEADS
KV_LORA = D_MODEL // 4
Q_LORA = 3 * KV_LORA // 2
MLA_NOPE = HEAD_DIM
MLA_ROPE = HEAD_DIM // 2
MLA_QK = MLA_NOPE + MLA_ROPE
MLA_V = HEAD_DIM
DIL_W = DIL_HEADS * HEAD_DIM
ROPE_THETA = 500000.0
ROT_DIM = HEAD_DIM // 4
DIL_PATTERNS = ((128, 1), (512, 4), (2048, 16))
BLOCK = 128
D_FF = ((8 * D_MODEL // 3 + 255) // 256) * 256
CONV_WIDTH = 3
EPS = 1e-6
IN_SPLITS = (Q_LORA, Q_LORA + KV_LORA, Q_LORA + KV_LORA + MLA_ROPE,
             Q_LORA + KV_LORA + MLA_ROPE + DIL_W, Q_LORA + KV_LORA + MLA_ROPE + 2 * DIL_W)
IN_WIDTH = Q_LORA + KV_LORA + MLA_ROPE + 3 * DIL_W

kernel_name = "hybrid_mla_dilated_stickbreak_convffn"


def rmsnorm(x, g):
    xf = x.astype(jnp.float32)
    y = xf * lax.rsqrt(jnp.mean(xf * xf, axis=-1, keepdims=True) + EPS)
    return (y * g.astype(jnp.float32)).astype(x.dtype)


def rope_tables(pos, dim):
    inv = ROPE_THETA ** (-jnp.arange(0, dim, 2, dtype=jnp.float32) / dim)
    ang = pos[:, None] * inv[None, :]
    return jnp.cos(ang), jnp.sin(ang)


def apply_rope(x, cos, sin):
    xf = x.astype(jnp.float32)
    half = xf.shape[-1] // 2
    x1, x2 = xf[..., :half], xf[..., half:]
    c, s = cos[None, :, None, :], sin[None, :, None, :]
    return jnp.concatenate([x1 * c - x2 * s, x2 * c + x1 * s], axis=-1).astype(x.dtype)


def causal_block_sweep(block_fn, q, k, v):
    S = q.shape[1]
    outs = []
    for n in range(S // BLOCK):
        lo, hi = n * BLOCK, (n + 1) * BLOCK
        outs.append(block_fn(q[:, lo:hi], k[:, :hi], v[:, :hi], lo))
    return jnp.concatenate(outs, axis=1)


def softmax_block(qi, kp, vp, q0):
    L = kp.shape[1]
    s = jnp.einsum('bqhd,bkhd->bhqk', qi, kp, preferred_element_type=jnp.float32)
    mask = jnp.arange(L)[None, :] <= (q0 + jnp.arange(BLOCK))[:, None]
    s = jnp.where(mask, s, -jnp.inf)
    m = jnp.max(s, axis=-1, keepdims=True)
    p = jnp.exp(s - m)
    l = jnp.sum(p, axis=-1)
    o = jnp.einsum('bhqk,bkhd->bqhd', p.astype(vp.dtype), vp,
                   preferred_element_type=jnp.float32)
    return (o / jnp.transpose(l, (0, 2, 1))[..., None]).astype(vp.dtype)


def stick_breaking_block(qi, kp, vp, q0):
    L = kp.shape[1]
    nk = L // BLOCK
    z = jnp.einsum('bqhd,bkhd->bhqk', qi, kp, preferred_element_type=jnp.float32)
    B_, H = z.shape[0], z.shape[1]
    mask = jnp.arange(L)[None, :] < (q0 + jnp.arange(BLOCK))[:, None]
    log_1mb = jnp.where(mask, -jax.nn.softplus(z), 0.0)
    lb = log_1mb.reshape(B_, H, BLOCK, nk, BLOCK)
    ar = jnp.arange(BLOCK)
    tri = (ar[:, None] >= ar[None, :]).astype(lb.dtype)
    r_in = jnp.einsum('bhqnj,js->bhqns', lb, tri)
    tot = r_in[..., 0]
    am = jnp.arange(nk)
    stri = (am[:, None] > am[None, :]).astype(lb.dtype)
    later = jnp.einsum('bhqm,mn->bhqn', tot, stri)
    log_a = z.reshape(B_, H, BLOCK, nk, BLOCK) + r_in + later[..., None]
    a = jnp.where(mask, jnp.exp(log_a).reshape(B_, H, BLOCK, L), 0.0)
    return jnp.einsum('bhqk,bkhd->bqhd', a.astype(vp.dtype), vp,
                      preferred_element_type=jnp.float32).astype(vp.dtype)


def dilated_branch(q, k, v, n_back, dil, scale):
    B_, S, H, D = q.shape
    L = S // dil
    nb = -(-L // BLOCK)
    Lp = nb * BLOCK

    def strided(x):
        return x.reshape(B_, L, dil, H, D).transpose(0, 2, 1, 3, 4)

    qs = jnp.pad(strided(q), ((0, 0), (0, 0), (0, Lp - L), (0, 0), (0, 0)))
    qs = qs.reshape(B_, dil, nb, BLOCK, H, D)

    def key_windows(x):
        xp = jnp.pad(strided(x), ((0, 0), (0, 0), (BLOCK, Lp - L), (0, 0), (0, 0)))
        xp = xp.reshape(B_, dil, nb + 1, BLOCK, H, D)
        return jnp.concatenate([xp[:, :, :-1], xp[:, :, 1:]], axis=3)

    kw, vw = key_windows(k), key_windows(v)
    s = jnp.einsum('brnqhd,brnkhd->brnqhk', qs, kw, preferred_element_type=jnp.float32) * scale
    i = jnp.arange(BLOCK)[:, None]
    j = jnp.arange(2 * BLOCK)[None, :]
    dist = i + BLOCK - j
    band = (dist >= 0) & (dist <= n_back)
    kidx = jnp.arange(nb)[:, None] * BLOCK + jnp.arange(2 * BLOCK)[None, :] - BLOCK
    mask = band[None, :, :] & (kidx >= 0)[:, None, :]
    s = jnp.where(mask[None, None, :, :, None, :], s, -jnp.inf)
    m = jnp.max(s, axis=-1, keepdims=True)
    p = jnp.exp(s - m)
    l = jnp.sum(p, axis=-1)
    o = jnp.einsum('brnqhk,brnkhd->brnqhd', p.astype(v.dtype), vw,
                   preferred_element_type=jnp.float32) / l[..., None]
    lse = m[..., 0] + jnp.log(l)
    o = o.reshape(B_, dil, Lp, H, D)[:, :, :L].transpose(0, 2, 1, 3, 4).reshape(B_, S, H, D)
    lse = lse.reshape(B_, dil, Lp, H)[:, :, :L].transpose(0, 2, 1, 3).reshape(B_, S, H)
    return o, lse


def dilated_mixture_attention(q, k, v):
    scale = q.shape[-1] ** -0.5
    outs, lses = [], []
    for window, dil in DIL_PATTERNS:
        o, lse = dilated_branch(q, k, v, window // dil, dil, scale)
        outs.append(o)
        lses.append(lse)
    w = jax.nn.softmax(jnp.stack(lses, axis=0), axis=0)
    out = jnp.sum(w[..., None] * jnp.stack(outs, axis=0), axis=0)
    return out.astype(v.dtype)


def mla_dilated_mixer(h, w_in, cq_norm, ckv_norm, w_uq, w_ukv, mla_q_norm, mla_k_norm,
                      dil_q_norm, dil_k_norm, w_o, cos_m, sin_m, cos_p, sin_p):
    B_, S, _ = h.shape
    proj = h @ w_in
    c_q, c_kv, k_rope, q_d, k_d, v_d = jnp.split(proj, IN_SPLITS, axis=-1)
    q = (rmsnorm(c_q, cq_norm) @ w_uq).reshape(B_, S, MLA_HEADS, MLA_QK)
    kv = (rmsnorm(c_kv, ckv_norm) @ w_ukv).reshape(B_, S, MLA_HEADS, MLA_NOPE + MLA_V)
    k_nope, v_m = kv[..., :MLA_NOPE], kv[..., MLA_NOPE:]
    k = jnp.concatenate([k_nope, jnp.broadcast_to(k_rope[:, :, None, :],
                                                  (B_, S, MLA_HEADS, MLA_ROPE))], axis=-1)
    q = rmsnorm(q, mla_q_norm)
    k = rmsnorm(k, mla_k_norm)
    q = jnp.concatenate([q[..., :MLA_NOPE], apply_rope(q[..., MLA_NOPE:], cos_m, sin_m)], axis=-1)
    k = jnp.concatenate([k[..., :MLA_NOPE], apply_rope(k[..., MLA_NOPE:], cos_m, sin_m)], axis=-1)
    o_a = causal_block_sweep(softmax_block, q * (MLA_QK ** -0.5), k, v_m)
    qd = rmsnorm(q_d.reshape(B_, S, DIL_HEADS, HEAD_DIM), dil_q_norm)
    kd = rmsnorm(k_d.reshape(B_, S, DIL_HEADS, HEAD_DIM), dil_k_norm)
    vd = v_d.reshape(B_, S, DIL_HEADS, HEAD_DIM)
    qd = jnp.concatenate([apply_rope(qd[..., :ROT_DIM], cos_p, sin_p), qd[..., ROT_DIM:]], axis=-1)
    kd = jnp.concatenate([apply_rope(kd[..., :ROT_DIM], cos_p, sin_p), kd[..., ROT_DIM:]], axis=-1)
    o_b = dilated_mixture_attention(qd, kd, vd)
    o = jnp.concatenate([o_a.reshape(B_, S, -1), o_b.reshape(B_, S, -1)], axis=-1)
    return o @ w_o


def stick_breaking_mixer(h, w_qkv, w_o):
    B_, S, _ = h.shape
    qkv = (h @ w_qkv).reshape(B_, S, 3, SB_HEADS, HEAD_DIM)
    q = qkv[:, :, 0] * (HEAD_DIM ** -0.5)
    o = causal_block_sweep(stick_breaking_block, q, qkv[:, :, 1], qkv[:, :, 2])
    return o.reshape(B_, S, -1) @ w_o


def conv_ffn(h, w_up, conv_w, conv_b, w_down):
    u = h @ w_up
    c = lax.conv_general_dilated(u, conv_w[:, None, :], window_strides=(1,),
                                 padding=((CONV_WIDTH - 1, 0),),
                                 dimension_numbers=('NWC', 'WIO', 'NWC'),
                                 feature_group_count=u.shape[-1]) + conv_b
    gate, val = c[..., :D_FF], c[..., D_FF:]
    return (jax.nn.silu(gate) * val) @ w_down


def setup_inputs(seed: int = 0) -> dict:
    key = jax.random.key(seed)
    ks = jax.random.split(key, 24)
    n_even = (DEPTH + 1) // 2
    n_odd = DEPTH // 2

    def w(k, shape, fan_in):
        return jax.random.normal(k, shape, jnp.float32) * fan_in ** -0.5

    def g(k, shape):
        return 1.0 + 0.02 * jax.random.normal(k, shape, jnp.float32)

    return {
        "x": jax.random.normal(ks[0], (BATCH, SEQ, D_MODEL), jnp.float32),
        "ev_attn_norm": g(ks[1], (n_even, D_MODEL)),
        "ev_w_in": w(ks[2], (n_even, D_MODEL, IN_WIDTH), D_MODEL),
        "ev_cq_norm": g(ks[3], (n_even, Q_LORA)),
        "ev_ckv_norm": g(ks[4], (n_even, KV_LORA)),
        "ev_w_uq": w(ks[5], (n_even, Q_LORA, MLA_HEADS * MLA_QK), Q_LORA),
        "ev_w_ukv": w(ks[6], (n_even, KV_LORA, MLA_HEADS * (MLA_NOPE + MLA_V)), KV_LORA),
        "ev_mla_q_norm": g(ks[7], (n_even, MLA_QK)),
        "ev_mla_k_norm": g(ks[8], (n_even, MLA_QK)),
        "ev_dil_q_norm": g(ks[9], (n_even, HEAD_DIM)),
        "ev_dil_k_norm": g(ks[10], (n_even, HEAD_DIM)),
        "ev_w_o": w(ks[11], (n_even, MLA_HEADS * MLA_V + DIL_W, D_MODEL), MLA_HEADS * MLA_V + DIL_W),
        "od_attn_norm": g(ks[12], (n_odd, D_MODEL)),
        "od_w_qkv": w(ks[13], (n_odd, D_MODEL, 3 * SB_HEADS * HEAD_DIM), D_MODEL),
        "od_w_o": w(ks[14], (n_odd, SB_HEADS * HEAD_DIM, D_MODEL), SB_HEADS * HEAD_DIM),
        "ffn_norm": g(ks[15], (DEPTH, D_MODEL)),
        "ffn_w_up": w(ks[16], (DEPTH, D_MODEL, 2 * D_FF), D_MODEL),
        "ffn_conv_w": w(ks[17], (DEPTH, CONV_WIDTH, 2 * D_FF), CONV_WIDTH),
        "ffn_conv_b": 0.02 * jax.random.normal(ks[18], (DEPTH, 2 * D_FF), jnp.float32),
        "ffn_w_down": w(ks[19], (DEPTH, D_FF, D_MODEL), D_FF),
    }


def reference(x, ev_attn_norm, ev_w_in, ev_cq_norm, ev_ckv_norm, ev_w_uq, ev_w_ukv,
              ev_mla_q_norm, ev_mla_k_norm, ev_dil_q_norm, ev_dil_k_norm, ev_w_o,
              od_attn_norm, od_w_qkv, od_w_o, ffn_norm, ffn_w_up, ffn_conv_w, ffn_conv_b,
              ffn_w_down):
    S = x.shape[1]
    pos = jnp.arange(S, dtype=jnp.float32)
    cos_m, sin_m = rope_tables(pos, MLA_ROPE)
    cos_p, sin_p = rope_tables(pos, ROT_DIM)
    for layer in range(DEPTH):
        i = layer // 2
        if layer % 2 == 0:
            h = rmsnorm(x, ev_attn_norm[i])
            x = x + mla_dilated_mixer(h, ev_w_in[i], ev_cq_norm[i], ev_ckv_norm[i], ev_w_uq[i],
                                      ev_w_ukv[i], ev_mla_q_norm[i], ev_mla_k_norm[i],
                                      ev_dil_q_norm[i], ev_dil_k_norm[i], ev_w_o[i],
                                      cos_m, sin_m, cos_p, sin_p)
        else:
            h = rmsnorm(x, od_attn_norm[i])
            x = x + stick_breaking_mixer(h, od_w_qkv[i], od_w_o[i])
        h = rmsnorm(x, ffn_norm[layer])
        x = x + conv_ffn(h, ffn_w_up[layer], ffn_conv_w[layer], ffn_conv_b[layer], ffn_w_down[layer])
    return x
```

```python
import functools

import numpy as np
import jax
import jax.numpy as jnp
from jax import lax
from jax.experimental import pallas as pl
from jax.experimental.pallas import tpu as pltpu

F32 = jnp.float32
BF16 = jnp.bfloat16

HEAD_DIM = 128
MLA_HEADS = 8
DIL_HEADS = 8
SB_HEADS = 16
MLA_NOPE = 128
MLA_ROPE = 64
MLA_QK = MLA_NOPE + MLA_ROPE
MLA_QK_PAD = 256
ROT_DIM = HEAD_DIM // 4
ROPE_THETA = 500000.0
EPS = 1e-6
DIL_PATTERNS = ((128, 1), (512, 4), (2048, 16))
DIL_SPAN = max(w for w, _ in DIL_PATTERNS)
CONV_WIDTH = 3
LOG2E = 1.4426950408889634

VMEM_LIMIT = 56 * 1024 * 1024


def _cparams(sem):
    return pltpu.CompilerParams(dimension_semantics=sem, vmem_limit_bytes=VMEM_LIMIT)


def _nt_dot(a, b):
    return lax.dot_general(a, b, (((1,), (1,)), ((), ())), preferred_element_type=F32)


def _rmsnorm_rows(x_ref, g_ref, out_ref, chunk=64):
    rows = x_ref.shape[0]

    def body(c, carry):
        r0 = pl.multiple_of(c * chunk, chunk)
        x = x_ref[pl.ds(r0, chunk), :].astype(F32)
        ms = jnp.mean(x * x, axis=-1, keepdims=True)
        out_ref[pl.ds(r0, chunk), :] = (x * lax.rsqrt(ms + EPS) * g_ref[...]).astype(out_ref.dtype)
        return carry

    lax.fori_loop(0, rows // chunk, body, 0)


def _norm_matmul_kernel(x_ref, g_ref, w_ref, cs_ref, o_ref, xn_ref):
    @pl.when(pl.program_id(1) == 0)
    def _():
        _rmsnorm_rows(x_ref, g_ref, xn_ref)

    acc = jnp.dot(xn_ref[...], w_ref[...], preferred_element_type=F32)
    o_ref[...] = (acc * cs_ref[...]).astype(o_ref.dtype)


def norm_matmul(x, g, w, col_scale, *, tm, tn):
    T, D = x.shape
    N = w.shape[1]
    return pl.pallas_call(
        _norm_matmul_kernel,
        grid=(T // tm, N // tn),
        in_specs=[
            pl.BlockSpec((tm, D), lambda i, j: (i, 0)),
            pl.BlockSpec((1, D), lambda i, j: (0, 0)),
            pl.BlockSpec((D, tn), lambda i, j: (0, j)),
            pl.BlockSpec((1, tn), lambda i, j: (0, j)),
        ],
        out_specs=pl.BlockSpec((tm, tn), lambda i, j: (i, j)),
        out_shape=jax.ShapeDtypeStruct((T, N), BF16),
        scratch_shapes=[pltpu.VMEM((tm, D), BF16)],
        compiler_params=_cparams(("parallel", "arbitrary")),
        name="norm_matmul",
    )(x, g, w, col_scale)


def _ffn_up_kernel(x_ref, g_ref, wg_ref, wv_ref, cwg_ref, cwv_ref, cbg_ref, cbv_ref, o_ref,
                   xn_ref, ug_ref, uv_ref, hg_ref, hv_ref, *, tiles_per_seq, chunk):
    i = pl.program_id(0)
    j = pl.program_id(1)
    tm = x_ref.shape[0]
    halo = 8

    @pl.when(j == 0)
    def _():
        _rmsnorm_rows(x_ref, g_ref, xn_ref)

    seq_start = (i % tiles_per_seq) == 0

    @pl.when(seq_start)
    def _():
        ug_ref[0:halo, :] = jnp.zeros((halo, ug_ref.shape[1]), F32)
        uv_ref[0:halo, :] = jnp.zeros((halo, uv_ref.shape[1]), F32)

    @pl.when(jnp.logical_not(seq_start))
    def _():
        ug_ref[0:halo, :] = hg_ref[j]
        uv_ref[0:halo, :] = hv_ref[j]

    xn = xn_ref[...]
    ug_ref[halo:, :] = jnp.dot(xn, wg_ref[...], preferred_element_type=F32)
    uv_ref[halo:, :] = jnp.dot(xn, wv_ref[...], preferred_element_type=F32)
    hg_ref[j] = ug_ref[tm:tm + halo, :]
    hv_ref[j] = uv_ref[tm:tm + halo, :]

    def conv(u_ref, cw_ref, cb_ref, r0):
        return (u_ref[pl.ds(r0 + halo - 2, chunk), :] * cw_ref[0:1, :]
                + u_ref[pl.ds(r0 + halo - 1, chunk), :] * cw_ref[1:2, :]
                + u_ref[pl.ds(r0 + halo, chunk), :] * cw_ref[2:3, :]
                + cb_ref[...])

    for c in range(tm // chunk):
        r0 = c * chunk
        cg = conv(ug_ref, cwg_ref, cbg_ref, r0)
        cv = conv(uv_ref, cwv_ref, cbv_ref, r0)
        o_ref[r0:r0 + chunk, :] = (cg * (1.0 / (1.0 + jnp.exp(-cg))) * cv).astype(o_ref.dtype)


def ffn_up(x, g, w_up, conv_w, conv_b, *, seq, tm, tn):
    T, D = x.shape
    dff = w_up.shape[1] // 2
    nj = dff // tn
    kern = functools.partial(_ffn_up_kernel, tiles_per_seq=seq // tm, chunk=64)
    return pl.pallas_call(
        kern,
        grid=(T // tm, nj),
        in_specs=[
            pl.BlockSpec((tm, D), lambda i, j: (i, 0)),
            pl.BlockSpec((1, D), lambda i, j: (0, 0)),
            pl.BlockSpec((D, tn), lambda i, j: (0, j)),
            pl.BlockSpec((D, tn), lambda i, j: (0, nj + j)),
            pl.BlockSpec((CONV_WIDTH, tn), lambda i, j: (0, j)),
            pl.BlockSpec((CONV_WIDTH, tn), lambda i, j: (0, nj + j)),
            pl.BlockSpec((1, tn), lambda i, j: (0, j)),
            pl.BlockSpec((1, tn), lambda i, j: (0, nj + j)),
        ],
        out_specs=pl.BlockSpec((tm, tn), lambda i, j: (i, j)),
        out_shape=jax.ShapeDtypeStruct((T, dff), BF16),
        scratch_shapes=[
            pltpu.VMEM((tm, D), BF16),
            pltpu.VMEM((tm + 8, tn), F32),
            pltpu.VMEM((tm + 8, tn), F32),
            pltpu.VMEM((nj, 8, tn), F32),
            pltpu.VMEM((nj, 8, tn), F32),
        ],
        compiler_params=_cparams(("arbitrary", "arbitrary")),
        name="ffn_up",
    )(x, g, w_up, w_up, conv_w, conv_w, conv_b, conv_b)


def _matmul_res_kernel(a_ref, w_ref, r_ref, o_ref, acc_ref):
    k = pl.program_id(2)

    @pl.when(k == 0)
    def _():
        acc_ref[...] = jnp.zeros(acc_ref.shape, F32)

    acc_ref[...] += jnp.dot(a_ref[...], w_ref[...], preferred_element_type=F32)

    @pl.when(k == pl.num_programs(2) - 1)
    def _():
        o_ref[...] = r_ref[...] + acc_ref[...]


def matmul_res(a, w, res, *, tm, tn, tk):
    T, K = a.shape
    N = w.shape[1]
    return pl.pallas_call(
        _matmul_res_kernel,
        grid=(T // tm, N // tn, K // tk),
        in_specs=[
            pl.BlockSpec((tm, tk), lambda i, j, k: (i, k)),
            pl.BlockSpec((tk, tn), lambda i, j, k: (k, j)),
            pl.BlockSpec((tm, tn), lambda i, j, k: (i, j)),
        ],
        out_specs=pl.BlockSpec((tm, tn), lambda i, j, k: (i, j)),
        out_shape=jax.ShapeDtypeStruct((T, N), F32),
        scratch_shapes=[pltpu.VMEM((tm, tn), F32)],
        compiler_params=_cparams(("parallel", "parallel", "arbitrary")),
        name="matmul_res",
    )(a, w, res)


def _rope128(y, c_ref, s1_ref, s2_ref, half):
    return (y * c_ref[...]
            + pltpu.roll(y, 128 - half, axis=1) * s1_ref[...]
            + pltpu.roll(y, half, axis=1) * s2_ref[...])


def _attn_prep_kernel(cq_ref, ckv_ref, kr_ref, qd_ref, kd_ref,
                      gcq_ref, gckv_ref, wuq_ref, wukv_ref, gq_ref, gk_ref, gdq_ref, gdk_ref,
                      cm_ref, s1m_ref, s2m_ref, cp_ref, s1p_ref, s2p_ref,
                      q_ref, k_ref, v_ref, qdo_ref, kdo_ref, cqn_ref, ckvn_ref):
    _rmsnorm_rows(cq_ref, gcq_ref, cqn_ref)
    _rmsnorm_rows(ckv_ref, gckv_ref, ckvn_ref)
    cqn = cqn_ref[...]
    ckvn = ckvn_ref[...]
    kr = kr_ref[...].astype(F32)
    ss_kr = jnp.sum(kr * kr, axis=-1, keepdims=True)
    gq = gq_ref[...]
    gk = gk_ref[...]
    for h in range(MLA_HEADS):
        qh = jnp.dot(cqn, wuq_ref[:, h * MLA_QK_PAD:(h + 1) * MLA_QK_PAD],
                     preferred_element_type=F32)
        r = lax.rsqrt(jnp.sum(qh * qh, axis=-1, keepdims=True) * (1.0 / MLA_QK) + EPS)
        y = qh * r * gq
        yr = _rope128(y[:, MLA_NOPE:], cm_ref, s1m_ref, s2m_ref, MLA_ROPE // 2)
        q_ref[h, :, 0:MLA_NOPE] = (y[:, :MLA_NOPE] * (MLA_QK ** -0.5)).astype(q_ref.dtype)
        q_ref[h, :, MLA_NOPE:] = (yr * (MLA_QK ** -0.5)).astype(q_ref.dtype)

        kvh = jnp.dot(ckvn, wukv_ref[:, h * 2 * HEAD_DIM:(h + 1) * 2 * HEAD_DIM],
                      preferred_element_type=F32)
        kn = kvh[:, :MLA_NOPE]
        rk = lax.rsqrt((jnp.sum(kn * kn, axis=-1, keepdims=True) + ss_kr) * (1.0 / MLA_QK) + EPS)
        k_ref[h, :, 0:MLA_NOPE] = (kn * rk * gk[:, :MLA_NOPE]).astype(k_ref.dtype)
        k_ref[h, :, MLA_NOPE:] = _rope128(kr * rk * gk[:, MLA_NOPE:], cm_ref, s1m_ref, s2m_ref,
                                          MLA_ROPE // 2).astype(k_ref.dtype)
        v_ref[h] = kvh[:, MLA_NOPE:].astype(v_ref.dtype)

    for h in range(DIL_HEADS):
        sl = slice(h * HEAD_DIM, (h + 1) * HEAD_DIM)
        for src, g_ref, dst in ((qd_ref, gdq_ref, qdo_ref), (kd_ref, gdk_ref, kdo_ref)):
            xh = src[:, sl].astype(F32)
            r = lax.rsqrt(jnp.mean(xh * xh, axis=-1, keepdims=True) + EPS)
            dst[:, sl] = _rope128(xh * r * g_ref[...], cp_ref, s1p_ref, s2p_ref,
                                  ROT_DIM // 2).astype(dst.dtype)


def attn_prep(proj, cols, gcq, gckv, wuq, wukv, gq, gk, gdq, gdk, tabs_m, tabs_p, *, batch, seq, tm):
    T = proj.shape[0]
    ts = seq // tm
    dq, dkv = cols["c_q"][1], cols["c_kv"][1]
    dil_w = DIL_HEADS * HEAD_DIM

    def colspec(name):
        off, width = cols[name]
        return pl.BlockSpec((tm, width), lambda i, o=off // width: (i, o))

    full = lambda a: pl.BlockSpec(a.shape, lambda i: (0,) * a.ndim)
    tab = pl.BlockSpec((tm, 128), lambda i: (i % ts, 0))
    head_out = lambda w: pl.BlockSpec((None, MLA_HEADS, tm, w), lambda i: (i // ts, 0, i % ts, 0))
    return pl.pallas_call(
        _attn_prep_kernel,
        grid=(T // tm,),
        in_specs=[colspec("c_q"), colspec("c_kv"), colspec("k_rope"), colspec("q_d"), colspec("k_d"),
                  full(gcq), full(gckv), full(wuq), full(wukv), full(gq), full(gk), full(gdq), full(gdk),
                  tab, tab, tab, tab, tab, tab],
        out_specs=[head_out(MLA_QK_PAD), head_out(MLA_QK_PAD), head_out(HEAD_DIM),
                   pl.BlockSpec((tm, dil_w), lambda i: (i, 0)),
                   pl.BlockSpec((tm, dil_w), lambda i: (i, 0))],
        out_shape=[jax.ShapeDtypeStruct((batch, MLA_HEADS, seq, MLA_QK_PAD), BF16),
                   jax.ShapeDtypeStruct((batch, MLA_HEADS, seq, MLA_QK_PAD), BF16),
                   jax.ShapeDtypeStruct((batch, MLA_HEADS, seq, HEAD_DIM), BF16),
                   jax.ShapeDtypeStruct((T, dil_w), BF16),
                   jax.ShapeDtypeStruct((T, dil_w), BF16)],
        scratch_shapes=[pltpu.VMEM((tm, dq), BF16), pltpu.VMEM((tm, dkv), BF16)],
        compiler_params=_cparams(("parallel",)),
        name="attn_prep",
    )(proj, proj, proj, proj, proj, gcq, gckv, wuq, wukv, gq, gk, gdq, gdk, *tabs_m, *tabs_p)


def _mla_flash_kernel(q_ref, k_ref, v_ref, o_ref):
    i = pl.program_id(2)
    tq = q_ref.shape[0]
    q = q_ref[...]

    def step(start, carry, mask):
        m, l, acc = carry
        k = k_ref[pl.ds(start, tq), :]
        v = v_ref[pl.ds(start, tq), :]
        s = _nt_dot(q, k)
        if mask is not None:
            s = jnp.where(mask, s, -1e30)
        m_new = jnp.maximum(m, jnp.max(s, axis=-1, keepdims=True))
        p = jnp.exp(s - m_new)
        alpha = jnp.exp(m - m_new)
        l = alpha * l + jnp.sum(p, axis=-1, keepdims=True)
        acc = alpha * acc + jnp.dot(p.astype(v.dtype), v, preferred_element_type=F32)
        return m_new, l, acc

    init = (jnp.full((tq, 1), -1e30, F32), jnp.zeros((tq, 1), F32), jnp.zeros((tq, HEAD_DIM), F32))
    carry = lax.fori_loop(
        0, i, lambda j, c: step(pl.multiple_of(j * tq, tq), c, None), init)
    row = lax.broadcasted_iota(jnp.int32, (tq, tq), 0)
    col = lax.broadcasted_iota(jnp.int32, (tq, tq), 1)
    m, l, acc = step(pl.multiple_of(i * tq, tq), carry, col <= row)
    o_ref[...] = (acc / l).astype(o_ref.dtype)


def mla_flash(q, k, v, *, tq):
    B, H, S, dk = q.shape
    return pl.pallas_call(
        _mla_flash_kernel,
        grid=(B, H, S // tq),
        in_specs=[
            pl.BlockSpec((None, None, tq, dk), lambda b, h, i: (b, h, i, 0)),
            pl.BlockSpec((None, None, S, dk), lambda b, h, i: (b, h, 0, 0)),
            pl.BlockSpec((None, None, S, HEAD_DIM), lambda b, h, i: (b, h, 0, 0)),
        ],
        out_specs=pl.BlockSpec((None, tq, HEAD_DIM), lambda b, h, i: (b, i, h)),
        out_shape=jax.ShapeDtypeStruct((B, S, H * HEAD_DIM), BF16),
        compiler_params=_cparams(("parallel", "parallel", "arbitrary")),
        name="mla_flash",
    )(q, k, v)


def _dil_attn_kernel(q_ref, k_ref, v_ref, cnt_ref, o_ref, *, scale):
    i = pl.program_id(2)
    tq = q_ref.shape[0]
    nchunk = cnt_ref.shape[0]
    q = q_ref[...]

    def step(j, carry):
        m, l, acc = carry
        start = pl.multiple_of((i - (nchunk - 1) + j) * tq, tq)
        k = k_ref[pl.ds(start, tq), :]
        v = v_ref[pl.ds(start, tq), :]
        cnt = cnt_ref[j]
        s = jnp.where(cnt > 0.0, _nt_dot(q, k) * scale, -1e30)
        m_new = jnp.maximum(m, jnp.max(s, axis=-1, keepdims=True))
        p = jnp.exp(s - m_new) * cnt
        alpha = jnp.exp(m - m_new)
        l = alpha * l + jnp.sum(p, axis=-1, keepdims=True)
        acc = alpha * acc + jnp.dot(p.astype(v.dtype), v, preferred_element_type=F32)
        return m_new, l, acc

    init = (jnp.full((tq, 1), -1e30, F32), jnp.zeros((tq, 1), F32), jnp.zeros((tq, HEAD_DIM), F32))
    m, l, acc = lax.fori_loop(jnp.maximum(nchunk - 1 - i, 0), nchunk, step, init)
    o_ref[...] = (acc / l).astype(o_ref.dtype)


def _dilated_counts(tq):
    nchunk = DIL_SPAN // tq + 1
    j = np.arange(nchunk)[:, None, None]
    r = np.arange(tq)[None, :, None]
    c = np.arange(tq)[None, None, :]
    d = (nchunk - 1 - j) * tq + r - c
    cnt = np.zeros(d.shape, np.float32)
    for window, dil in DIL_PATTERNS:
        cnt += ((d >= 0) & (d <= window) & (d % dil == 0)).astype(np.float32)
    return jnp.asarray(cnt)


def dil_attn(qd, kd, proj, v_col0, *, tq):
    B, S, _ = qd.shape
    cnt = _dilated_counts(tq)
    vb = v_col0 // HEAD_DIM
    kern = functools.partial(_dil_attn_kernel, scale=HEAD_DIM ** -0.5)
    return pl.pallas_call(
        kern,
        grid=(B, DIL_HEADS, S // tq),
        in_specs=[
            pl.BlockSpec((None, tq, HEAD_DIM), lambda b, h, i: (b, i, h)),
            pl.BlockSpec((None, S, HEAD_DIM), lambda b, h, i: (b, 0, h)),
            pl.BlockSpec((None, S, HEAD_DIM), lambda b, h, i: (b, 0, vb + h)),
            pl.BlockSpec(cnt.shape, lambda b, h, i: (0, 0, 0)),
        ],
        out_specs=pl.BlockSpec((None, tq, HEAD_DIM), lambda b, h, i: (b, i, h)),
        out_shape=jax.ShapeDtypeStruct((B, S, DIL_HEADS * HEAD_DIM), BF16),
        compiler_params=_cparams(("parallel", "parallel", "arbitrary")),
        name="dil_attn",
    )(qd, kd, proj, cnt)


def _out_proj2_kernel(a_ref, b_ref, w_ref, r_ref, o_ref):
    ka = a_ref.shape[1]
    acc = jnp.dot(a_ref[...], w_ref[0:ka, :], preferred_element_type=F32)
    acc += jnp.dot(b_ref[...], w_ref[ka:, :], preferred_element_type=F32)
    o_ref[...] = r_ref[...] + acc


def out_proj2(a, b, w, res, *, tm, tn):
    T, ka = a.shape
    kb = b.shape[1]
    N = w.shape[1]
    return pl.pallas_call(
        _out_proj2_kernel,
        grid=(T // tm, N // tn),
        in_specs=[
            pl.BlockSpec((tm, ka), lambda i, j: (i, 0)),
            pl.BlockSpec((tm, kb), lambda i, j: (i, 0)),
            pl.BlockSpec((ka + kb, tn), lambda i, j: (0, j)),
            pl.BlockSpec((tm, tn), lambda i, j: (i, j)),
        ],
        out_specs=pl.BlockSpec((tm, tn), lambda i, j: (i, j)),
        out_shape=jax.ShapeDtypeStruct((T, N), F32),
        compiler_params=_cparams(("parallel", "parallel")),
        name="out_proj2",
    )(a, b, w, res)


def _sb_attn_kernel(q_ref, k_ref, v_ref, tri_ref, o_ref, *, w):
    i = pl.program_id(2)
    tq = q_ref.shape[0]
    q = q_ref[...]
    tri = tri_ref[...]

    def step(start, carry, mask):
        c, acc = carry
        k = k_ref[pl.ds(start, w), :]
        v = v_ref[pl.ds(start, w), :]
        z = _nt_dot(q, k)
        sp = jnp.maximum(z, 0.0) + jnp.log2(1.0 + jnp.exp2(-jnp.abs(z)))
        if mask is not None:
            sp = jnp.where(mask, sp, 0.0)
        r = jnp.dot(sp.astype(tri.dtype), tri, preferred_element_type=F32)
        a = jnp.exp2(z - r - c)
        if mask is not None:
            a = jnp.where(mask, a, 0.0)
        acc = acc + jnp.dot(a.astype(v.dtype), v, preferred_element_type=F32)
        return c + r[:, 0:1], acc

    carry = (jnp.zeros((tq, 1), F32), jnp.zeros((tq, HEAD_DIM), F32))
    q0 = i * tq
    row = lax.broadcasted_iota(jnp.int32, (tq, w), 0)
    col = lax.broadcasted_iota(jnp.int32, (tq, w), 1)
    for d in reversed(range(tq // w)):
        carry = step(pl.multiple_of(q0 + d * w, w), carry, (col + d * w) < row)
    n_off = q0 // w

    def body(t, carry):
        return step(pl.multiple_of((n_off - 1 - t) * w, w), carry, None)

    _, acc = lax.fori_loop(0, n_off, body, carry)
    o_ref[...] = acc.astype(o_ref.dtype)


def sb_attn(qkv, *, tq, w):
    B, S, _ = qkv.shape
    H = SB_HEADS
    ar = np.arange(w)
    tri = jnp.asarray((ar[:, None] >= ar[None, :]).astype(np.float32), dtype=BF16)
    kern = functools.partial(_sb_attn_kernel, w=w)
    return pl.pallas_call(
        kern,
        grid=(B, H, S // tq),
        in_specs=[
            pl.BlockSpec((None, tq, HEAD_DIM), lambda b, h, i: (b, i, h)),
            pl.BlockSpec((None, S, HEAD_DIM), lambda b, h, i: (b, 0, H + h)),
            pl.BlockSpec((None, S, HEAD_DIM), lambda b, h, i: (b, 0, 2 * H + h)),
            pl.BlockSpec((w, w), lambda b, h, i: (0, 0)),
        ],
        out_specs=pl.BlockSpec((None, tq, HEAD_DIM), lambda b, h, i: (b, i, h)),
        out_shape=jax.ShapeDtypeStruct((B, S, H * HEAD_DIM), BF16),
        compiler_params=_cparams(("parallel", "parallel", "arbitrary")),
        name="sb_attn",
    )(qkv, qkv, qkv, tri)


def _rope_lane_tables(seq, dim, tail):
    half = dim // 2
    pos = jnp.arange(seq, dtype=F32)
    inv = ROPE_THETA ** (-jnp.arange(0, dim, 2, dtype=F32) / dim)
    ang = pos[:, None] * inv[None, :]
    cos, sin = jnp.cos(ang), jnp.sin(ang)
    z = jnp.zeros((seq, 128 - dim), F32)
    zh = jnp.zeros((seq, half), F32)
    c = jnp.concatenate([cos, cos, z + tail], axis=1)
    s1 = jnp.concatenate([-sin, zh, z], axis=1)
    s2 = jnp.concatenate([zh, sin, z], axis=1)
    return c, s1, s2


def _pick(n, prefs):
    for p in prefs:
        if n % p == 0:
            return p
    return n


def kernel(x, ev_attn_norm, ev_w_in, ev_cq_norm, ev_ckv_norm, ev_w_uq, ev_w_ukv, ev_mla_q_norm,
           ev_mla_k_norm, ev_dil_q_norm, ev_dil_k_norm, ev_w_o, od_attn_norm, od_w_qkv, od_w_o,
           ffn_norm, ffn_w_up, ffn_conv_w, ffn_conv_b, ffn_w_down):
    B, S, D = x.shape
    T = B * S
    depth = ffn_norm.shape[0]
    q_lora = ev_cq_norm.shape[1]
    kv_lora = ev_ckv_norm.shape[1]
    dil_w = DIL_HEADS * HEAD_DIM
    tm = _pick(S, (1024, 512, 256))
    row = lambda v: v.reshape(1, -1).astype(F32)

    tabs_m = _rope_lane_tables(S, MLA_ROPE, 0.0)
    tabs_p = _rope_lane_tables(S, ROT_DIM, 1.0)

    xf = x.reshape(T, D)
    for layer in range(depth):
        li = layer // 2
        if layer % 2 == 0:
            o0 = q_lora + kv_lora + MLA_ROPE
            w_in = ev_w_in[li]
            zeros = lambda n: jnp.zeros((D, n), w_in.dtype)
            pad = (-(3 * dil_w + kv_lora + 128) - q_lora) % q_lora
            w_in_p = jnp.concatenate([
                w_in[:, o0:o0 + 3 * dil_w], w_in[:, q_lora:q_lora + kv_lora],
                w_in[:, q_lora + kv_lora:o0], zeros(128 - MLA_ROPE), zeros(pad), w_in[:, :q_lora]],
                axis=1).astype(BF16)
            cols = {"q_d": (0, dil_w), "k_d": (dil_w, dil_w), "v_d": (2 * dil_w, dil_w),
                    "c_kv": (3 * dil_w, kv_lora), "k_rope": (3 * dil_w + kv_lora, 128),
                    "c_q": (3 * dil_w + kv_lora + 128 + pad, q_lora)}
            n_in = w_in_p.shape[1]
            proj = norm_matmul(xf, row(ev_attn_norm[li]), w_in_p, jnp.ones((1, n_in), F32),
                               tm=tm, tn=_pick(n_in, (768, 512, 256, 128)))

            wuq = ev_w_uq[li].reshape(q_lora, MLA_HEADS, MLA_QK)
            wuq = jnp.pad(wuq, ((0, 0), (0, 0), (0, MLA_QK_PAD - MLA_QK)))
            wuq = wuq.reshape(q_lora, MLA_HEADS * MLA_QK_PAD).astype(BF16)
            gpad = lambda g: jnp.pad(g, (0, MLA_QK_PAD - MLA_QK)).reshape(1, -1).astype(F32)
            q_m, k_m, v_m, q_dl, k_dl = attn_prep(
                proj, cols, row(ev_cq_norm[li]), row(ev_ckv_norm[li]), wuq,
                ev_w_ukv[li].astype(BF16), gpad(ev_mla_q_norm[li]), gpad(ev_mla_k_norm[li]),
                row(ev_dil_q_norm[li]), row(ev_dil_k_norm[li]), tabs_m, tabs_p,
                batch=B, seq=S, tm=_pick(S, (512, 256)))
            o_a = mla_flash(q_m, k_m, v_m, tq=_pick(S, (512, 256)))
            o_b = dil_attn(q_dl.reshape(B, S, dil_w), k_dl.reshape(B, S, dil_w),
                           proj.reshape(B, S, n_in), cols["v_d"][0], tq=256)
            xf = out_proj2(o_a.reshape(T, -1), o_b.reshape(T, -1), ev_w_o[li].astype(BF16), xf,
                           tm=tm, tn=_pick(D, (1024, 512)))
        else:
            hd = SB_HEADS * HEAD_DIM
            cs = jnp.concatenate([jnp.full((1, hd), HEAD_DIM ** -0.5 * LOG2E, F32),
                                  jnp.ones((1, 2 * hd), F32)], axis=1)
            qkv = norm_matmul(xf, row(od_attn_norm[li]), od_w_qkv[li].astype(BF16), cs,
                              tm=tm, tn=_pick(3 * hd, (1024, 512)))
            o = sb_attn(qkv.reshape(B, S, 3 * hd), tq=_pick(S, (512, 256)), w=256)
            xf = matmul_res(o.reshape(T, hd), od_w_o[li].astype(BF16), xf,
                            tm=tm, tn=_pick(D, (1024, 512)), tk=_pick(hd, (1024, 512)))

        dff = ffn_w_down.shape[1]
        g = ffn_up(xf, row(ffn_norm[layer]), ffn_w_up[layer].astype(BF16),
                   ffn_conv_w[layer].astype(F32), ffn_conv_b[layer].reshape(1, -1).astype(F32),
                   seq=S, tm=tm, tn=_pick(dff, (512, 256)))
        xf = matmul_res(g, ffn_w_down[layer].astype(BF16), xf,
                        tm=tm, tn=_pick(D, (512,)), tk=_pick(dff, (2816, 1408, 512)))
    return xf.reshape(B, S, D)
```

```python
import functools

import numpy as np
import jax
import jax.numpy as jnp
from jax import lax
from jax.experimental import pallas as pl
from jax.experimental.pallas import tpu as pltpu

F32 = jnp.float32
BF16 = jnp.bfloat16

HEAD_DIM = 128
MLA_HEADS = 8
DIL_HEADS = 8
SB_HEADS = 16
MLA_NOPE = 128
MLA_ROPE = 64
MLA_QK = MLA_NOPE + MLA_ROPE
MLA_QK_PAD = 256
ROT_DIM = HEAD_DIM // 4
ROPE_THETA = 500000.0
EPS = 1e-6
DIL_PATTERNS = ((128, 1), (512, 4), (2048, 16))
DIL_SPAN = max(w for w, _ in DIL_PATTERNS)
CONV_WIDTH = 3
LOG2E = 1.4426950408889634
Q_SCALE_MLA = MLA_QK ** -0.5 * LOG2E

VMEM_LIMIT = 56 * 1024 * 1024


def _cparams(sem):
    return pltpu.CompilerParams(dimension_semantics=sem, vmem_limit_bytes=VMEM_LIMIT)


def _nt_dot(a, b):
    return lax.dot_general(a, b, (((1,), (1,)), ((), ())), preferred_element_type=F32)


def _rmsnorm_rows(x_ref, g_ref, out_ref, chunk=64):
    rows = x_ref.shape[0]

    def body(c, carry):
        r0 = pl.multiple_of(c * chunk, chunk)
        x = x_ref[pl.ds(r0, chunk), :].astype(F32)
        ms = jnp.mean(x * x, axis=-1, keepdims=True)
        out_ref[pl.ds(r0, chunk), :] = (x * lax.rsqrt(ms + EPS) * g_ref[...]).astype(out_ref.dtype)
        return carry

    lax.fori_loop(0, rows // chunk, body, 0)


def _norm_matmul_kernel(x_ref, g_ref, w_ref, cs_ref, o_ref, xn_ref):
    @pl.when(pl.program_id(1) == 0)
    def _():
        _rmsnorm_rows(x_ref, g_ref, xn_ref)

    acc = jnp.dot(xn_ref[...], w_ref[...], preferred_element_type=F32)
    o_ref[...] = (acc * cs_ref[...]).astype(o_ref.dtype)


def norm_matmul(x, g, w, col_scale, *, tm, tn):
    T, D = x.shape
    N = w.shape[1]
    return pl.pallas_call(
        _norm_matmul_kernel,
        grid=(T // tm, N // tn),
        in_specs=[
            pl.BlockSpec((tm, D), lambda i, j: (i, 0)),
            pl.BlockSpec((1, D), lambda i, j: (0, 0)),
            pl.BlockSpec((D, tn), lambda i, j: (0, j)),
            pl.BlockSpec((1, tn), lambda i, j: (0, j)),
        ],
        out_specs=pl.BlockSpec((tm, tn), lambda i, j: (i, j)),
        out_shape=jax.ShapeDtypeStruct((T, N), BF16),
        scratch_shapes=[pltpu.VMEM((tm, D), BF16)],
        compiler_params=_cparams(("parallel", "arbitrary")),
        name="norm_matmul",
    )(x, g, w, col_scale)


def _ffn_up_kernel(x_ref, g_ref, wg_ref, wv_ref, cwg_ref, cwv_ref, cbg_ref, cbv_ref, o_ref,
                   xn_ref, ug_ref, uv_ref, hg_ref, hv_ref, *, tiles_per_seq, chunk):
    i = pl.program_id(0)
    j = pl.program_id(1)
    tm = x_ref.shape[0]
    halo = 8

    @pl.when(j == 0)
    def _():
        _rmsnorm_rows(x_ref, g_ref, xn_ref)

    seq_start = (i % tiles_per_seq) == 0

    @pl.when(seq_start)
    def _():
        ug_ref[0:halo, :] = jnp.zeros((halo, ug_ref.shape[1]), F32)
        uv_ref[0:halo, :] = jnp.zeros((halo, uv_ref.shape[1]), F32)

    @pl.when(jnp.logical_not(seq_start))
    def _():
        ug_ref[0:halo, :] = hg_ref[j]
        uv_ref[0:halo, :] = hv_ref[j]

    xn = xn_ref[...]
    ug_ref[halo:, :] = jnp.dot(xn, wg_ref[...], preferred_element_type=F32)
    uv_ref[halo:, :] = jnp.dot(xn, wv_ref[...], preferred_element_type=F32)
    hg_ref[j] = ug_ref[tm:tm + halo, :]
    hv_ref[j] = uv_ref[tm:tm + halo, :]

    def conv(u_ref, cw_ref, cb_ref, r0):
        return (u_ref[pl.ds(r0 + halo - 2, chunk), :] * cw_ref[0:1, :]
                + u_ref[pl.ds(r0 + halo - 1, chunk), :] * cw_ref[1:2, :]
                + u_ref[pl.ds(r0 + halo, chunk), :] * cw_ref[2:3, :]
                + cb_ref[...])

    for c in range(tm // chunk):
        r0 = c * chunk
        cg = conv(ug_ref, cwg_ref, cbg_ref, r0)
        cv = conv(uv_ref, cwv_ref, cbv_ref, r0)
        o_ref[r0:r0 + chunk, :] = (cg * (1.0 / (1.0 + jnp.exp(-cg))) * cv).astype(o_ref.dtype)


def ffn_up(x, g, w_up, conv_w, conv_b, *, seq, tm, tn):
    T, D = x.shape
    dff = w_up.shape[1] // 2
    nj = dff // tn
    kern = functools.partial(_ffn_up_kernel, tiles_per_seq=seq // tm, chunk=64)
    return pl.pallas_call(
        kern,
        grid=(T // tm, nj),
        in_specs=[
            pl.BlockSpec((tm, D), lambda i, j: (i, 0)),
            pl.BlockSpec((1, D), lambda i, j: (0, 0)),
            pl.BlockSpec((D, tn), lambda i, j: (0, j)),
            pl.BlockSpec((D, tn), lambda i, j: (0, nj + j)),
            pl.BlockSpec((CONV_WIDTH, tn), lambda i, j: (0, j)),
            pl.BlockSpec((CONV_WIDTH, tn), lambda i, j: (0, nj + j)),
            pl.BlockSpec((1, tn), lambda i, j: (0, j)),
            pl.BlockSpec((1, tn), lambda i, j: (0, nj + j)),
        ],
        out_specs=pl.BlockSpec((tm, tn), lambda i, j: (i, j)),
        out_shape=jax.ShapeDtypeStruct((T, dff), BF16),
        scratch_shapes=[
            pltpu.VMEM((tm, D), BF16),
            pltpu.VMEM((tm + 8, tn), F32),
            pltpu.VMEM((tm + 8, tn), F32),
            pltpu.VMEM((nj, 8, tn), F32),
            pltpu.VMEM((nj, 8, tn), F32),
        ],
        compiler_params=_cparams(("arbitrary", "arbitrary")),
        name="ffn_up",
    )(x, g, w_up, w_up, conv_w, conv_w, conv_b, conv_b)


def _matmul_res_kernel(a_ref, w_ref, r_ref, o_ref, acc_ref):
    k = pl.program_id(2)

    @pl.when(k == 0)
    def _():
        acc_ref[...] = jnp.zeros(acc_ref.shape, F32)

    acc_ref[...] += jnp.dot(a_ref[...], w_ref[...], preferred_element_type=F32)

    @pl.when(k == pl.num_programs(2) - 1)
    def _():
        o_ref[...] = r_ref[...] + acc_ref[...]


def matmul_res(a, w, res, *, tm, tn, tk):
    T, K = a.shape
    N = w.shape[1]
    return pl.pallas_call(
        _matmul_res_kernel,
        grid=(T // tm, N // tn, K // tk),
        in_specs=[
            pl.BlockSpec((tm, tk), lambda i, j, k: (i, k)),
            pl.BlockSpec((tk, tn), lambda i, j, k: (k, j)),
            pl.BlockSpec((tm, tn), lambda i, j, k: (i, j)),
        ],
        out_specs=pl.BlockSpec((tm, tn), lambda i, j, k: (i, j)),
        out_shape=jax.ShapeDtypeStruct((T, N), F32),
        scratch_shapes=[pltpu.VMEM((tm, tn), F32)],
        compiler_params=_cparams(("parallel", "parallel", "arbitrary")),
        name="matmul_res",
    )(a, w, res)


def _rope128(y, c_ref, s1_ref, s2_ref, half):
    return (y * c_ref[...]
            + pltpu.roll(y, 128 - half, axis=1) * s1_ref[...]
            + pltpu.roll(y, half, axis=1) * s2_ref[...])


def _attn_prep_kernel(cq_ref, ckv_ref, kr_ref, qd_ref, kd_ref,
                      gcq_ref, gckv_ref, wuq_ref, wukv_ref, gq_ref, gk_ref, gdq_ref, gdk_ref,
                      cm_ref, s1m_ref, s2m_ref, cp_ref, s1p_ref, s2p_ref,
                      q_ref, k_ref, v_ref, qdo_ref, kdo_ref, cqn_ref, ckvn_ref):
    _rmsnorm_rows(cq_ref, gcq_ref, cqn_ref)
    _rmsnorm_rows(ckv_ref, gckv_ref, ckvn_ref)
    cqn = cqn_ref[...]
    ckvn = ckvn_ref[...]
    kr = kr_ref[...].astype(F32)
    ss_kr = jnp.sum(kr * kr, axis=-1, keepdims=True)
    gq = gq_ref[...]
    gk = gk_ref[...]
    for h in range(MLA_HEADS):
        qh = jnp.dot(cqn, wuq_ref[:, h * MLA_QK_PAD:(h + 1) * MLA_QK_PAD],
                     preferred_element_type=F32)
        r = lax.rsqrt(jnp.sum(qh * qh, axis=-1, keepdims=True) * (1.0 / MLA_QK) + EPS)
        y = qh * r * gq
        yr = _rope128(y[:, MLA_NOPE:], cm_ref, s1m_ref, s2m_ref, MLA_ROPE // 2)
        q_ref[h, :, 0:MLA_NOPE] = (y[:, :MLA_NOPE] * Q_SCALE_MLA).astype(q_ref.dtype)
        q_ref[h, :, MLA_NOPE:] = (yr * Q_SCALE_MLA).astype(q_ref.dtype)

        kvh = jnp.dot(ckvn, wukv_ref[:, h * 2 * HEAD_DIM:(h + 1) * 2 * HEAD_DIM],
                      preferred_element_type=F32)
        kn = kvh[:, :MLA_NOPE]
        rk = lax.rsqrt((jnp.sum(kn * kn, axis=-1, keepdims=True) + ss_kr) * (1.0 / MLA_QK) + EPS)
        k_ref[h, :, 0:MLA_NOPE] = (kn * rk * gk[:, :MLA_NOPE]).astype(k_ref.dtype)
        k_ref[h, :, MLA_NOPE:] = _rope128(kr * rk * gk[:, MLA_NOPE:], cm_ref, s1m_ref, s2m_ref,
                                          MLA_ROPE // 2).astype(k_ref.dtype)
        v_ref[h] = kvh[:, MLA_NOPE:].astype(v_ref.dtype)

    for h in range(DIL_HEADS):
        sl = slice(h * HEAD_DIM, (h + 1) * HEAD_DIM)
        for src, g_ref, dst in ((qd_ref, gdq_ref, qdo_ref), (kd_ref, gdk_ref, kdo_ref)):
            xh = src[:, sl].astype(F32)
            r = lax.rsqrt(jnp.mean(xh * xh, axis=-1, keepdims=True) + EPS)
            dst[:, sl] = _rope128(xh * r * g_ref[...], cp_ref, s1p_ref, s2p_ref,
                                  ROT_DIM // 2).astype(dst.dtype)


def attn_prep(proj, cols, gcq, gckv, wuq, wukv, gq, gk, gdq, gdk, tabs_m, tabs_p, *, batch, seq, tm):
    T = proj.shape[0]
    ts = seq // tm
    dq, dkv = cols["c_q"][1], cols["c_kv"][1]
    dil_w = DIL_HEADS * HEAD_DIM

    def colspec(name):
        off, width = cols[name]
        return pl.BlockSpec((tm, width), lambda i, o=off // width: (i, o))

    full = lambda a: pl.BlockSpec(a.shape, lambda i: (0,) * a.ndim)
    tab = pl.BlockSpec((tm, 128), lambda i: (i % ts, 0))
    head_out = lambda w: pl.BlockSpec((None, MLA_HEADS, tm, w), lambda i: (i // ts, 0, i % ts, 0))
    return pl.pallas_call(
        _attn_prep_kernel,
        grid=(T // tm,),
        in_specs=[colspec("c_q"), colspec("c_kv"), colspec("k_rope"), colspec("q_d"), colspec("k_d"),
                  full(gcq), full(gckv), full(wuq), full(wukv), full(gq), full(gk), full(gdq), full(gdk),
                  tab, tab, tab, tab, tab, tab],
        out_specs=[head_out(MLA_QK_PAD), head_out(MLA_QK_PAD), head_out(HEAD_DIM),
                   pl.BlockSpec((tm, dil_w), lambda i: (i, 0)),
                   pl.BlockSpec((tm, dil_w), lambda i: (i, 0))],
        out_shape=[jax.ShapeDtypeStruct((batch, MLA_HEADS, seq, MLA_QK_PAD), BF16),
                   jax.ShapeDtypeStruct((batch, MLA_HEADS, seq, MLA_QK_PAD), BF16),
                   jax.ShapeDtypeStruct((batch, MLA_HEADS, seq, HEAD_DIM), BF16),
                   jax.ShapeDtypeStruct((T, dil_w), BF16),
                   jax.ShapeDtypeStruct((T, dil_w), BF16)],
        scratch_shapes=[pltpu.VMEM((tm, dq), BF16), pltpu.VMEM((tm, dkv), BF16)],
        compiler_params=_cparams(("parallel",)),
        name="attn_prep",
    )(proj, proj, proj, proj, proj, gcq, gckv, wuq, wukv, gq, gk, gdq, gdk, *tabs_m, *tabs_p)


def _softmax_update(s, v, m_ref, l_ref, acc_ref, weight=None):
    m = m_ref[...]
    m_new = jnp.maximum(m, jnp.max(s, axis=-1, keepdims=True))
    p = jnp.exp2(s - m_new)
    if weight is not None:
        p = p * weight
    alpha = jnp.exp2(m - m_new)
    l_ref[...] = alpha * l_ref[...] + jnp.sum(p, axis=-1, keepdims=True)
    acc_ref[...] = alpha * acc_ref[...] + jnp.dot(p.astype(v.dtype), v, preferred_element_type=F32)
    m_ref[...] = m_new


def _softmax_init(m_ref, l_ref, acc_ref):
    m_ref[...] = jnp.full(m_ref.shape, -1e30, F32)
    l_ref[...] = jnp.zeros(l_ref.shape, F32)
    acc_ref[...] = jnp.zeros(acc_ref.shape, F32)


def _mla_flash_kernel(q_ref, k_ref, v_ref, o_ref, s_ref, m_ref, l_ref, acc_ref, *, tk):
    i = pl.program_id(2)
    tq = q_ref.shape[0]
    nd = tq // tk
    _softmax_init(m_ref, l_ref, acc_ref)

    def scores(j, slot):
        s_ref[slot] = _nt_dot(q_ref[...], k_ref[pl.ds(pl.multiple_of(j * tk, tk), tk), :])

    def update(j, slot, mask=None):
        s = s_ref[slot]
        if mask is not None:
            s = jnp.where(mask, s, -1e30)
        _softmax_update(s, v_ref[pl.ds(pl.multiple_of(j * tk, tk), tk), :], m_ref, l_ref, acc_ref)

    n_off = i * nd
    scores(0, 0)

    def body(p, carry):
        u = 2 * p
        scores(u + 1, 1)
        update(u, 0)
        scores(u + 2, 0)
        update(u + 1, 1)
        return carry

    lax.fori_loop(0, n_off // 2, body, 0)
    row = lax.broadcasted_iota(jnp.int32, (tq, tk), 0)
    col = lax.broadcasted_iota(jnp.int32, (tq, tk), 1)
    for d in range(nd):
        if d + 1 < nd:
            scores(n_off + d + 1, (d + 1) % 2)
        update(n_off + d, d % 2, (col + d * tk) <= row)
    o_ref[...] = (acc_ref[...] / l_ref[...]).astype(o_ref.dtype)


def mla_flash(q, k, v, *, tq, tk):
    B, H, S, dk = q.shape
    assert tq % (2 * tk) == 0
    return pl.pallas_call(
        functools.partial(_mla_flash_kernel, tk=tk),
        grid=(B, H, S // tq),
        in_specs=[
            pl.BlockSpec((None, None, tq, dk), lambda b, h, i: (b, h, i, 0)),
            pl.BlockSpec((None, None, S, dk), lambda b, h, i: (b, h, 0, 0)),
            pl.BlockSpec((None, None, S, HEAD_DIM), lambda b, h, i: (b, h, 0, 0)),
        ],
        out_specs=pl.BlockSpec((None, tq, HEAD_DIM), lambda b, h, i: (b, i, h)),
        out_shape=jax.ShapeDtypeStruct((B, S, H * HEAD_DIM), BF16),
        scratch_shapes=[pltpu.VMEM((2, tq, tk), F32), pltpu.VMEM((tq, 1), F32),
                        pltpu.VMEM((tq, 1), F32), pltpu.VMEM((tq, HEAD_DIM), F32)],
        compiler_params=_cparams(("parallel", "parallel", "arbitrary")),
        name="mla_flash",
    )(q, k, v)


def _dil_attn_kernel(q_ref, k_ref, v_ref, cnt_ref, o_ref, s_ref, m_ref, l_ref, acc_ref, *, scale):
    i = pl.program_id(2)
    tq = q_ref.shape[0]
    nchunk = cnt_ref.shape[0]
    _softmax_init(m_ref, l_ref, acc_ref)

    def start(j):
        return pl.multiple_of(jnp.maximum(i - (nchunk - 1) + j, 0) * tq, tq)

    def scores(j, slot):
        s_ref[slot] = _nt_dot(q_ref[...], k_ref[pl.ds(start(j), tq), :])

    def update(j, slot):
        weight = cnt_ref[j]
        if j < nchunk - 1:
            weight = jnp.where(i - (nchunk - 1) + j >= 0, weight, 0.0)
        s = jnp.where(weight > 0.0, s_ref[slot] * scale, -1e30)
        _softmax_update(s, v_ref[pl.ds(start(j), tq), :], m_ref, l_ref, acc_ref, weight=weight)

    order = list(reversed(range(nchunk)))
    scores(order[0], 0)
    for n, j in enumerate(order):
        if n + 1 < nchunk:
            scores(order[n + 1], (n + 1) % 2)
        update(j, n % 2)
    o_ref[...] = (acc_ref[...] / l_ref[...]).astype(o_ref.dtype)


def _dilated_counts(tq):
    nchunk = DIL_SPAN // tq + 1
    j = np.arange(nchunk)[:, None, None]
    r = np.arange(tq)[None, :, None]
    c = np.arange(tq)[None, None, :]
    d = (nchunk - 1 - j) * tq + r - c
    cnt = np.zeros(d.shape, np.float32)
    for window, dil in DIL_PATTERNS:
        cnt += ((d >= 0) & (d <= window) & (d % dil == 0)).astype(np.float32)
    return jnp.asarray(cnt)


def dil_attn(qd, kd, proj, v_col0, *, tq):
    B, S, _ = qd.shape
    cnt = _dilated_counts(tq)
    vb = v_col0 // HEAD_DIM
    kern = functools.partial(_dil_attn_kernel, scale=HEAD_DIM ** -0.5 * LOG2E)
    return pl.pallas_call(
        kern,
        grid=(B, DIL_HEADS, S // tq),
        in_specs=[
            pl.BlockSpec((None, tq, HEAD_DIM), lambda b, h, i: (b, i, h)),
            pl.BlockSpec((None, S, HEAD_DIM), lambda b, h, i: (b, 0, h)),
            pl.BlockSpec((None, S, HEAD_DIM), lambda b, h, i: (b, 0, vb + h)),
            pl.BlockSpec(cnt.shape, lambda b, h, i: (0, 0, 0)),
        ],
        out_specs=pl.BlockSpec((None, tq, HEAD_DIM), lambda b, h, i: (b, i, h)),
        out_shape=jax.ShapeDtypeStruct((B, S, DIL_HEADS * HEAD_DIM), BF16),
        scratch_shapes=[pltpu.VMEM((2, tq, tq), F32), pltpu.VMEM((tq, 1), F32),
                        pltpu.VMEM((tq, 1), F32), pltpu.VMEM((tq, HEAD_DIM), F32)],
        compiler_params=_cparams(("parallel", "parallel", "arbitrary")),
        name="dil_attn",
    )(qd, kd, proj, cnt)


def _out_proj2_kernel(a_ref, b_ref, w_ref, r_ref, o_ref):
    ka = a_ref.shape[1]
    acc = jnp.dot(a_ref[...], w_ref[0:ka, :], preferred_element_type=F32)
    acc += jnp.dot(b_ref[...], w_ref[ka:, :], preferred_element_type=F32)
    o_ref[...] = r_ref[...] + acc


def out_proj2(a, b, w, res, *, tm, tn):
    T, ka = a.shape
    kb = b.shape[1]
    N = w.shape[1]
    return pl.pallas_call(
        _out_proj2_kernel,
        grid=(T // tm, N // tn),
        in_specs=[
            pl.BlockSpec((tm, ka), lambda i, j: (i, 0)),
            pl.BlockSpec((tm, kb), lambda i, j: (i, 0)),
            pl.BlockSpec((ka + kb, tn), lambda i, j: (0, j)),
            pl.BlockSpec((tm, tn), lambda i, j: (i, j)),
        ],
        out_specs=pl.BlockSpec((tm, tn), lambda i, j: (i, j)),
        out_shape=jax.ShapeDtypeStruct((T, N), F32),
        compiler_params=_cparams(("parallel", "parallel")),
        name="out_proj2",
    )(a, b, w, res)


def _sb_attn_kernel(q_ref, k_ref, v_ref, tri_ref, o_ref, z_ref, r_ref, c_ref, acc_ref, *, w):
    i = pl.program_id(2)
    tq = q_ref.shape[0]
    sign = jnp.uint32(0x80000000)

    def logits(start, slot, mask=None):
        z = _nt_dot(q_ref[...], k_ref[pl.ds(start, w), :])
        z_ref[slot] = z
        neg_abs = pltpu.bitcast(pltpu.bitcast(z, jnp.uint32) | sign, F32)
        sp = jnp.maximum(z, 0.0) + jnp.log(1.0 + jnp.exp2(neg_abs)) * LOG2E
        if mask is not None:
            sp = jnp.where(mask, sp, 0.0)
        r_ref[slot] = jnp.dot(sp.astype(tri_ref.dtype), tri_ref[...], preferred_element_type=F32)

    def accumulate(start, slot, mask=None):
        r = r_ref[slot]
        c = c_ref[...]
        a = jnp.exp2(z_ref[slot] - r - c)
        if mask is not None:
            a = jnp.where(mask, a, 0.0)
        acc_ref[...] += jnp.dot(a.astype(v_ref.dtype), v_ref[pl.ds(start, w), :],
                                preferred_element_type=F32)
        c_ref[...] = c + r[:, 0:1]

    c_ref[...] = jnp.zeros(c_ref.shape, F32)
    acc_ref[...] = jnp.zeros(acc_ref.shape, F32)
    q0 = i * tq
    row = lax.broadcasted_iota(jnp.int32, (tq, w), 0)
    col = lax.broadcasted_iota(jnp.int32, (tq, w), 1)
    for d in reversed(range(tq // w)):
        start = pl.multiple_of(q0 + d * w, w)
        mask = (col + d * w) < row
        logits(start, d % 2, mask)
        accumulate(start, d % 2, mask)

    n_off = q0 // w
    chunk_start = lambda u: pl.multiple_of(jnp.maximum(n_off - 1 - u, 0) * w, w)
    logits(chunk_start(0), 0)

    def body(p, carry):
        u = 2 * p
        logits(chunk_start(u + 1), 1)
        accumulate(chunk_start(u), 0)
        logits(chunk_start(u + 2), 0)
        accumulate(chunk_start(u + 1), 1)
        return carry

    lax.fori_loop(0, n_off // 2, body, 0)
    o_ref[...] = acc_ref[...].astype(o_ref.dtype)


def sb_attn(qkv, *, tq, w):
    B, S, _ = qkv.shape
    H = SB_HEADS
    assert tq % (2 * w) == 0
    ar = np.arange(w)
    tri = jnp.asarray((ar[:, None] >= ar[None, :]).astype(np.float32), dtype=BF16)
    kern = functools.partial(_sb_attn_kernel, w=w)
    return pl.pallas_call(
        kern,
        grid=(B, H, S // tq),
        in_specs=[
            pl.BlockSpec((None, tq, HEAD_DIM), lambda b, h, i: (b, i, h)),
            pl.BlockSpec((None, S, HEAD_DIM), lambda b, h, i: (b, 0, H + h)),
            pl.BlockSpec((None, S, HEAD_DIM), lambda b, h, i: (b, 0, 2 * H + h)),
            pl.BlockSpec((w, w), lambda b, h, i: (0, 0)),
        ],
        out_specs=pl.BlockSpec((None, tq, HEAD_DIM), lambda b, h, i: (b, i, h)),
        out_shape=jax.ShapeDtypeStruct((B, S, H * HEAD_DIM), BF16),
        scratch_shapes=[pltpu.VMEM((2, tq, w), F32), pltpu.VMEM((2, tq, w), F32),
                        pltpu.VMEM((tq, 1), F32), pltpu.VMEM((tq, HEAD_DIM), F32)],
        compiler_params=_cparams(("parallel", "parallel", "arbitrary")),
        name="sb_attn",
    )(qkv, qkv, qkv, tri)


def _rope_lane_tables(seq, dim, tail):
    half = dim // 2
    pos = jnp.arange(seq, dtype=F32)
    inv = ROPE_THETA ** (-jnp.arange(0, dim, 2, dtype=F32) / dim)
    ang = pos[:, None] * inv[None, :]
    cos, sin = jnp.cos(ang), jnp.sin(ang)
    z = jnp.zeros((seq, 128 - dim), F32)
    zh = jnp.zeros((seq, half), F32)
    c = jnp.concatenate([cos, cos, z + tail], axis=1)
    s1 = jnp.concatenate([-sin, zh, z], axis=1)
    s2 = jnp.concatenate([zh, sin, z], axis=1)
    return c, s1, s2


def _pick(n, prefs):
    for p in prefs:
        if n % p == 0:
            return p
    return n


def kernel(x, ev_attn_norm, ev_w_in, ev_cq_norm, ev_ckv_norm, ev_w_uq, ev_w_ukv, ev_mla_q_norm,
           ev_mla_k_norm, ev_dil_q_norm, ev_dil_k_norm, ev_w_o, od_attn_norm, od_w_qkv, od_w_o,
           ffn_norm, ffn_w_up, ffn_conv_w, ffn_conv_b, ffn_w_down):
    B, S, D = x.shape
    T = B * S
    depth = ffn_norm.shape[0]
    q_lora = ev_cq_norm.shape[1]
    kv_lora = ev_ckv_norm.shape[1]
    dil_w = DIL_HEADS * HEAD_DIM
    tm = _pick(S, (1024, 512, 256))
    row = lambda v: v.reshape(1, -1).astype(F32)

    tabs_m = _rope_lane_tables(S, MLA_ROPE, 0.0)
    tabs_p = _rope_lane_tables(S, ROT_DIM, 1.0)

    xf = x.reshape(T, D)
    for layer in range(depth):
        li = layer // 2
        if layer % 2 == 0:
            o0 = q_lora + kv_lora + MLA_ROPE
            w_in = ev_w_in[li]
            zeros = lambda n: jnp.zeros((D, n), w_in.dtype)
            pad = (-(3 * dil_w + kv_lora + 128) - q_lora) % q_lora
            w_in_p = jnp.concatenate([
                w_in[:, o0:o0 + 3 * dil_w], w_in[:, q_lora:q_lora + kv_lora],
                w_in[:, q_lora + kv_lora:o0], zeros(128 - MLA_ROPE), zeros(pad), w_in[:, :q_lora]],
                axis=1).astype(BF16)
            cols = {"q_d": (0, dil_w), "k_d": (dil_w, dil_w), "v_d": (2 * dil_w, dil_w),
                    "c_kv": (3 * dil_w, kv_lora), "k_rope": (3 * dil_w + kv_lora, 128),
                    "c_q": (3 * dil_w + kv_lora + 128 + pad, q_lora)}
            n_in = w_in_p.shape[1]
            proj = norm_matmul(xf, row(ev_attn_norm[li]), w_in_p, jnp.ones((1, n_in), F32),
                               tm=tm, tn=_pick(n_in, (768, 512, 256, 128)))

            wuq = ev_w_uq[li].reshape(q_lora, MLA_HEADS, MLA_QK)
            wuq = jnp.pad(wuq, ((0, 0), (0, 0), (0, MLA_QK_PAD - MLA_QK)))
            wuq = wuq.reshape(q_lora, MLA_HEADS * MLA_QK_PAD).astype(BF16)
            gpad = lambda g: jnp.pad(g, (0, MLA_QK_PAD - MLA_QK)).reshape(1, -1).astype(F32)
            q_m, k_m, v_m, q_dl, k_dl = attn_prep(
                proj, cols, row(ev_cq_norm[li]), row(ev_ckv_norm[li]), wuq,
                ev_w_ukv[li].astype(BF16), gpad(ev_mla_q_norm[li]), gpad(ev_mla_k_norm[li]),
                row(ev_dil_q_norm[li]), row(ev_dil_k_norm[li]), tabs_m, tabs_p,
                batch=B, seq=S, tm=_pick(S, (512, 256)))
            o_a = mla_flash(q_m, k_m, v_m, tq=_pick(S, (1024,)), tk=512)
            o_b = dil_attn(q_dl.reshape(B, S, dil_w), k_dl.reshape(B, S, dil_w),
                           proj.reshape(B, S, n_in), cols["v_d"][0], tq=_pick(S, (512, 256)))
            xf = out_proj2(o_a.reshape(T, -1), o_b.reshape(T, -1), ev_w_o[li].astype(BF16), xf,
                           tm=tm, tn=_pick(D, (1024, 512)))
        else:
            hd = SB_HEADS * HEAD_DIM
            cs = jnp.concatenate([jnp.full((1, hd), HEAD_DIM ** -0.5 * LOG2E, F32),
                                  jnp.ones((1, 2 * hd), F32)], axis=1)
            qkv = norm_matmul(xf, row(od_attn_norm[li]), od_w_qkv[li].astype(BF16), cs,
                              tm=tm, tn=_pick(3 * hd, (1024, 512)))
            o = sb_attn(qkv.reshape(B, S, 3 * hd), tq=_pick(S, (1024, 512)), w=256)
            xf = matmul_res(o.reshape(T, hd), od_w_o[li].astype(BF16), xf,
                            tm=tm, tn=_pick(D, (1024, 512)), tk=_pick(hd, (1024, 512)))

        dff = ffn_w_down.shape[1]
        g = ffn_up(xf, row(ffn_norm[layer]), ffn_w_up[layer].astype(BF16),
                   ffn_conv_w[layer].astype(F32), ffn_conv_b[layer].reshape(1, -1).astype(F32),
                   seq=S, tm=tm, tn=_pick(dff, (512, 256)))
        xf = matmul_res(g, ffn_w_down[layer].astype(BF16), xf,
                        tm=tm, tn=_pick(D, (512,)), tk=_pick(dff, (2816, 1408, 512)))
    return xf.reshape(B, S, D)
```
